```python
import math
import jax, jax.numpy as jnp
from jax import lax
import numpy as np

D_MODEL = 4096
BATCH = 2
SEQ = 8192
DEPTH = 1

N_META = 16
CHUNK = 128
Q_BLOCK = 128
H_RET = 16
RET_DK = D_MODEL // H_RET
RET_DV = D_MODEL // H_RET
RET_W = H_RET * RET_DV
DIFF_HD = 128
H_DIFF = D_MODEL // (2 * DIFF_HD)
DIFF_W = H_DIFF * 2 * DIFF_HD
D_FF = ((8 * D_MODEL + 3 * 256 - 1) // (3 * 256)) * 256
EPS = 1e-6
SPLIT_SIZES = (H_RET * RET_DK, H_RET * RET_DK, RET_W, RET_W,
               DIFF_W, DIFF_W, DIFF_W,
               D_MODEL, D_MODEL)
D_IN = sum(SPLIT_SIZES)
SPLIT_POINTS = tuple(int(v) for v in np.cumsum(SPLIT_SIZES)[:-1])

kernel_name = "hybrid_retention_diffattn_gated_encoder"


def _rmsnorm(x, g):
    x32 = x.astype(jnp.float32)
    y = x32 * lax.rsqrt(jnp.mean(x32 * x32, axis=-1, keepdims=True) + EPS)
    return (y * g.astype(jnp.float32)).astype(x.dtype)


def _unit_rms(x32):
    return x32 * lax.rsqrt(jnp.mean(x32 * x32, axis=-1, keepdims=True) + EPS)


def _lambda_init(layer):
    return 0.8 - 0.6 * math.exp(-0.3 * layer)


def _retention_scan(q, k, v, log_g, include_diag):
    b, h, t, dk = q.shape
    dv = v.shape[-1]
    n = t // CHUNK
    idx = jnp.arange(CHUNK, dtype=jnp.float32)
    dist = idx[:, None] - idx[None, :]
    keep = (dist >= 0) if include_diag else (dist > 0)
    intra = jnp.where(keep[None], jnp.exp(log_g[:, None, None] * jnp.maximum(dist, 0.0)[None]), 0.0)
    q_dec = jnp.exp(log_g[:, None] * (idx + 1.0)[None])[..., None]
    k_dec = jnp.exp(log_g[:, None] * (CHUNK - 1.0 - idx)[None])[..., None]
    c_dec = jnp.exp(log_g * CHUNK)[:, None, None]

    def chunks(a):
        return a.reshape(b, h, n, CHUNK, a.shape[-1]).transpose(2, 0, 1, 3, 4)

    def step(state, qkv):
        qc, kc, vc = qkv
        scores = jnp.einsum('bhid,bhjd->bhij', qc, kc) * intra
        out = (jnp.einsum('bhij,bhjv->bhiv', scores, vc)
               + jnp.einsum('bhid,bhdv->bhiv', qc * q_dec, state))
        state = state * c_dec + jnp.einsum('bhjd,bhjv->bhdv', kc * k_dec, vc)
        return state, out

    s0 = jnp.zeros((b, h, dk, dv), jnp.float32)
    _, out = lax.scan(step, s0, (chunks(q), chunks(k), chunks(v)))
    return out.transpose(1, 2, 0, 3, 4).reshape(b, h, t, dv)


def _retention_branch(q, k, v, g, log_decay_param, w_branch):
    b, l, _ = q.shape
    pad = CHUNK - N_META

    def heads(a, d):
        a = a.astype(jnp.float32).reshape(b, l, H_RET, d).transpose(0, 2, 1, 3)
        return jnp.pad(a, ((0, 0), (0, 0), (pad, 0), (0, 0)))

    qh = heads(q, RET_DK)
    kh = heads(k, RET_DK) * (RET_DK ** -0.5)
    vh = heads(v, RET_DV)
    log_g = -jnp.exp(log_decay_param.astype(jnp.float32))
    fwd = _retention_scan(qh, kh, vh, log_g[0], True)
    rev = lambda a: a[:, :, ::-1]
    bwd = rev(_retention_scan(rev(qh), rev(kh), rev(vh), log_g[1], False))
    y = _unit_rms((fwd + bwd)[:, :, pad:])
    y = y.transpose(0, 2, 1, 3).reshape(b, l, RET_W).astype(q.dtype)
    return (jax.nn.silu(g) * y) @ w_branch


def _diff_attn_branch(q, k, v, lam_params, subln_g, lam_init, w_branch):
    b, l, _ = q.shape
    qh = q.reshape(b, l, H_DIFF, 2, DIFF_HD).transpose(0, 2, 3, 1, 4) * (DIFF_HD ** -0.5)
    kh = k.reshape(b, l, H_DIFF, 2, DIFF_HD).transpose(0, 2, 3, 1, 4)
    vh = v.reshape(b, l, H_DIFF, 2 * DIFF_HD).transpose(0, 2, 1, 3)
    n_blk = -(-l // Q_BLOCK)
    qh = jnp.pad(qh, ((0, 0), (0, 0), (0, 0), (0, n_blk * Q_BLOCK - l), (0, 0)))
    lp = lam_params.astype(jnp.float32)
    lam = jnp.exp(jnp.sum(lp[0] * lp[1])) - jnp.exp(jnp.sum(lp[2] * lp[3])) + lam_init
    slopes = 2.0 ** (-8.0 * jnp.arange(1, H_DIFF + 1, dtype=jnp.float32) / H_DIFF)
    kpos = jnp.arange(l, dtype=jnp.float32)

    def block(i):
        qb = lax.dynamic_slice_in_dim(qh, i * Q_BLOCK, Q_BLOCK, axis=3)
        s = jnp.einsum('bhmqd,bhmkd->bhmqk', qb, kh).astype(jnp.float32)
        qpos = (i * Q_BLOCK + jnp.arange(Q_BLOCK)).astype(jnp.float32)
        alibi = -slopes[:, None, None] * jnp.abs(qpos[:, None] - kpos[None, :])[None]
        p = jax.nn.softmax(s + alibi[None, :, None], axis=-1)
        a = p[:, :, 0] - lam * p[:, :, 1]
        return jnp.einsum('bhqk,bhkv->bhqv', a.astype(vh.dtype), vh)

    o = lax.map(block, jnp.arange(n_blk))
    o = o.transpose(1, 2, 0, 3, 4).reshape(b, H_DIFF, n_blk * Q_BLOCK, 2 * DIFF_HD)[:, :, :l]
    o = _unit_rms(o.astype(jnp.float32)) * subln_g.astype(jnp.float32) * (1.0 - lam_init)
    o = o.transpose(0, 2, 1, 3).reshape(b, l, DIFF_W).astype(q.dtype)
    return o @ w_branch


def setup_inputs(seed: int = 0) -> dict:
    key = jax.random.key(seed)
    ks = jax.random.split(key, 16)
    f32 = jnp.float32
    nrm = lambda k, shape, s: jax.random.normal(k, shape, f32) * s
    heads = jnp.arange(H_RET, dtype=f32)
    base = jnp.log(-jnp.log1p(-(2.0 ** (-5.0 - heads))))
    return {
        "x": nrm(ks[0], (BATCH, SEQ, D_MODEL), 1.0),
        "meta_tokens": nrm(ks[1], (N_META, D_MODEL), 1.0),
        "norm_mix_g": 1.0 + nrm(ks[2], (DEPTH, D_MODEL), 0.02),
        "w_in": nrm(ks[3], (DEPTH, D_MODEL, D_IN), D_MODEL ** -0.5),
        "ret_log_decay": base[None, None, :] + nrm(ks[4], (DEPTH, 2, H_RET), 0.01),
        "diff_lambda": nrm(ks[5], (DEPTH, 4, DIFF_HD), 0.1),
        "diff_subln_g": 1.0 + nrm(ks[6], (DEPTH, 2 * DIFF_HD), 0.02),
        "w_branch_ret": nrm(ks[7], (DEPTH, RET_W, D_MODEL), RET_W ** -0.5),
        "w_branch_diff": nrm(ks[8], (DEPTH, DIFF_W, D_MODEL), DIFF_W ** -0.5),
        "w_out": nrm(ks[9], (DEPTH, D_MODEL, D_MODEL), D_MODEL ** -0.5),
        "norm_ffn_g": 1.0 + nrm(ks[10], (DEPTH, D_MODEL), 0.02),
        "w_ffn_gate": nrm(ks[11], (DEPTH, D_MODEL, D_FF), D_MODEL ** -0.5),
        "w_ffn_up": nrm(ks[12], (DEPTH, D_MODEL, D_FF), D_MODEL ** -0.5),
        "w_ffn_down": nrm(ks[13], (DEPTH, D_FF, D_MODEL), D_FF ** -0.5),
        "norm_final_g": 1.0 + nrm(ks[14], (D_MODEL,), 0.02),
    }


def reference(x, meta_tokens, norm_mix_g, w_in, ret_log_decay, diff_lambda, diff_subln_g,
              w_branch_ret, w_branch_diff, w_out, norm_ffn_g, w_ffn_gate, w_ffn_up, w_ffn_down,
              norm_final_g):
    b = x.shape[0]
    meta = jnp.broadcast_to(meta_tokens[None].astype(x.dtype), (b, N_META, D_MODEL))
    h = jnp.concatenate([meta, x], axis=1)
    for layer in range(DEPTH):
        u = _rmsnorm(h, norm_mix_g[layer])
        proj = u @ w_in[layer]
        rq, rk, rv, rg, dq, dk, dv, ga, gb = jnp.split(proj, SPLIT_POINTS, axis=-1)
        y_ret = _retention_branch(rq, rk, rv, rg, ret_log_decay[layer], w_branch_ret[layer])
        y_diff = _diff_attn_branch(dq, dk, dv, diff_lambda[layer], diff_subln_g[layer],
                                   _lambda_init(layer), w_branch_diff[layer])
        mixed = jax.nn.sigmoid(ga) * y_ret + jax.nn.sigmoid(gb) * y_diff
        h = h + mixed @ w_out[layer]
        u = _rmsnorm(h, norm_ffn_g[layer])
        h = h + (jax.nn.silu(u @ w_ffn_gate[layer]) * (u @ w_ffn_up[layer])) @ w_ffn_down[layer]
    return _rmsnorm(h, norm_final_g)[:, N_META:]
```

```python
import functools
import math

import jax
import jax.numpy as jnp
import numpy as np
from jax import lax
from jax.experimental import pallas as pl
from jax.experimental.pallas import tpu as pltpu

N_META = 16
CHUNK = 128
H_RET = 16
H_DIFF = 16
HEAD_W = 256
DIFF_HD = 128
EPS = 1e-6
NEG_BIG = -1e30

V7X_VMEM_BYTES = 64 * 1024 * 1024
VMEM_LIMIT = V7X_VMEM_BYTES - 8 * 1024 * 1024


def _largest_divisor(n, target, multiple):
    d = (min(n, target) // multiple) * multiple
    while d > multiple and n % d:
        d -= multiple
    assert d >= multiple and n % d == 0, (n, target, multiple)
    return d


def _params(*semantics):
    return pltpu.CompilerParams(dimension_semantics=semantics, vmem_limit_bytes=VMEM_LIMIT)


def _rmsnorm_rows_kernel(x_ref, g_ref, o_ref):
    x = x_ref[...].astype(jnp.float32)
    y = x * lax.rsqrt(jnp.mean(x * x, axis=-1, keepdims=True) + EPS)
    o_ref[...] = (y * g_ref[...]).astype(o_ref.dtype)


def _rmsnorm_mix_kernel(x_ref, meta_ref, g_ref, o_ref, *, n_x_blocks):
    i = pl.program_id(0)

    def norm(v):
        y = v * lax.rsqrt(jnp.mean(v * v, axis=-1, keepdims=True) + EPS)
        return (y * g_ref[...]).astype(o_ref.dtype)

    @pl.when(i < n_x_blocks)
    def _():
        o_ref[...] = norm(x_ref[...])

    @pl.when(i >= n_x_blocks)
    def _():
        o_ref[...] = norm(meta_ref[...])


def _rmsnorm_mix(x2d, meta_chunk, g, batch):
    rows, d = x2d.shape
    n_x_blocks = rows // CHUNK
    return pl.pallas_call(
        functools.partial(_rmsnorm_mix_kernel, n_x_blocks=n_x_blocks),
        grid=(n_x_blocks + batch,),
        in_specs=[
            pl.BlockSpec((CHUNK, d), lambda i: (jnp.minimum(i, n_x_blocks - 1), 0)),
            pl.BlockSpec((CHUNK, d), lambda i: (0, 0)),
            pl.BlockSpec((1, d), lambda i: (0, 0)),
        ],
        out_specs=pl.BlockSpec((CHUNK, d), lambda i: (i, 0)),
        out_shape=jax.ShapeDtypeStruct((rows + batch * CHUNK, d), jnp.bfloat16),
        compiler_params=_params("arbitrary"),
        name="rmsnorm_mix",
    )(x2d, meta_chunk, g.reshape(1, d))


def _rmsnorm_rows(x2d, g, out_dtype):
    rows, d = x2d.shape
    br = _largest_divisor(rows, 512, 8)
    return pl.pallas_call(
        _rmsnorm_rows_kernel,
        grid=(rows // br,),
        in_specs=[pl.BlockSpec((br, d), lambda i: (i, 0)),
                  pl.BlockSpec((1, d), lambda i: (0, 0))],
        out_specs=pl.BlockSpec((br, d), lambda i: (i, 0)),
        out_shape=jax.ShapeDtypeStruct((rows, d), out_dtype),
        compiler_params=_params("arbitrary"),
        name="rmsnorm_rows",
    )(x2d, g.reshape(1, d))


def _sigmoid(x):
    return 1.0 / (1.0 + jnp.exp(-x))


def _mm_kernel(*refs, mode, nk):
    if mode == "swiglu":
        a_ref, b0_ref, b1_ref, o_ref = refs
        a = a_ref[...]
        gate = jnp.dot(a, b0_ref[...], preferred_element_type=jnp.float32)
        up = jnp.dot(a, b1_ref[...], preferred_element_type=jnp.float32)
        o_ref[...] = (gate * _sigmoid(gate) * up).astype(o_ref.dtype)
        return

    n_extra = {"plain": 0, "gate": 1, "add": 1, "gate_add": 2}[mode]
    a_ref, b_ref = refs[:2]
    extra = refs[2:2 + n_extra]
    o_ref = refs[2 + n_extra]

    def epilogue(acc):
        if mode == "plain":
            r = acc
        elif mode == "gate":
            r = _sigmoid(extra[0][...].astype(jnp.float32)) * acc
        elif mode == "add":
            r = extra[0][...].astype(jnp.float32) + acc
        else:
            r = extra[1][...].astype(jnp.float32) + _sigmoid(extra[0][...].astype(jnp.float32)) * acc
        o_ref[...] = r.astype(o_ref.dtype)

    part = jnp.dot(a_ref[...], b_ref[...], preferred_element_type=jnp.float32)
    if nk == 1:
        epilogue(part)
        return

    acc_ref = refs[3 + n_extra]
    k = pl.program_id(2)

    @pl.when(k == 0)
    def _():
        acc_ref[...] = part

    @pl.when(jnp.logical_and(k > 0, k < nk - 1))
    def _():
        acc_ref[...] += part

    @pl.when(k == nk - 1)
    def _():
        epilogue(acc_ref[...] + part)


def _matmul(a, bs, out_dtype, *, mode="plain", extras=(), bm, bn, bk=None, name):
    m, kdim = a.shape
    n = bs[0].shape[1]
    bk = kdim if bk is None else bk
    nk = kdim // bk
    assert m % bm == 0 and n % bn == 0 and kdim % bk == 0
    in_specs = [pl.BlockSpec((bm, bk), lambda i, j, k: (i, k))]
    in_specs += [pl.BlockSpec((bk, bn), lambda i, j, k: (k, j)) for _ in bs]
    for _, off in extras:
        in_specs.append(pl.BlockSpec((bm, bn), lambda i, j, k, off=off: (i, off + j)))
    scratch = [pltpu.VMEM((bm, bn), jnp.float32)] if nk > 1 else []
    return pl.pallas_call(
        functools.partial(_mm_kernel, mode=mode, nk=nk),
        grid=(m // bm, n // bn, nk),
        in_specs=in_specs,
        out_specs=pl.BlockSpec((bm, bn), lambda i, j, k: (i, j)),
        out_shape=jax.ShapeDtypeStruct((m, n), out_dtype),
        scratch_shapes=scratch,
        compiler_params=_params("parallel", "parallel", "arbitrary"),
        name=name,
    )(a, *bs, *[e for e, _ in extras])


def _ret_kernel(*refs, heads_per_step, backward):
    if backward:
        decay_ref, q_ref, k_ref, v_ref, gate_ref, fwd_ref, o_ref, s_ref = refs
    else:
        decay_ref, q_ref, k_ref, v_ref, o_ref, s_ref = refs
    step = pl.program_id(2)

    @pl.when(step == 0)
    def _():
        s_ref[...] = jnp.zeros_like(s_ref)

    row = lax.broadcasted_iota(jnp.int32, (CHUNK, CHUNK), 0).astype(jnp.float32)
    col = lax.broadcasted_iota(jnp.int32, (CHUNK, CHUNK), 1).astype(jnp.float32)
    dist = (col - row) if backward else (row - col)
    keep = (dist > 0) if backward else (dist >= 0)
    pos = lax.broadcasted_iota(jnp.int32, (CHUNK, HEAD_W), 0).astype(jnp.float32)
    k_scale = HEAD_W ** -0.5

    for g in range(heads_per_step):
        cols = slice(g * HEAD_W, (g + 1) * HEAD_W)
        log_g = -jnp.exp(decay_ref[0, g])
        log_g_row = log_g[0:1, :]
        intra = jnp.where(keep, jnp.exp(log_g_row[:, :CHUNK] * jnp.maximum(dist, 0.0)), 0.0) * k_scale
        if backward:
            q_dec = jnp.exp(log_g_row * (CHUNK - pos))
            k_dec = jnp.exp(log_g_row * pos) * k_scale
        else:
            q_dec = jnp.exp(log_g_row * (pos + 1.0))
            k_dec = jnp.exp(log_g_row * (CHUNK - 1.0 - pos)) * k_scale
        c_dec = jnp.exp(log_g_row * float(CHUNK))

        q = q_ref[:, cols]
        k = k_ref[:, cols]
        v = v_ref[:, cols]
        state = s_ref[g]
        scores = lax.dot_general(q, k, (((1,), (1,)), ((), ())),
                                 preferred_element_type=jnp.float32) * intra
        out = jnp.dot(scores.astype(jnp.bfloat16), v, preferred_element_type=jnp.float32)
        out += jnp.dot((q.astype(jnp.float32) * q_dec).astype(jnp.bfloat16),
                       state.astype(jnp.bfloat16), preferred_element_type=jnp.float32)
        k_w = (k.astype(jnp.float32) * k_dec).astype(jnp.bfloat16)
        s_ref[g] = state * c_dec + lax.dot_general(k_w, v, (((0,), (0,)), ((), ())),
                                                   preferred_element_type=jnp.float32)
        if backward:
            tot = out + fwd_ref[:, cols]
            y = tot * lax.rsqrt(jnp.mean(tot * tot, axis=-1, keepdims=True) + EPS)
            gate = gate_ref[:, cols].astype(jnp.float32)
            o_ref[:, cols] = (gate * _sigmoid(gate) * y).astype(o_ref.dtype)
        else:
            @pl.when(step > 0)
            def _():
                o_ref[:, cols] = out


def _retention(proj, decay_tab, *, batch, seq, fwd=None):
    backward = fwd is not None
    n_x_chunks = seq // CHUNK
    hps = 4
    width = hps * HEAD_W
    seg_blocks = (H_RET * HEAD_W) // width
    meta_block0 = batch * n_x_chunks

    if backward:
        n_steps = n_x_chunks

        def row_block(b, c):
            return b * n_x_chunks + (n_x_chunks - 1 - c)

        out_rows = row_block
    else:
        n_steps = n_x_chunks + 1

        def row_block(b, c):
            return jnp.where(c == 0, meta_block0 + b, b * n_x_chunks + c - 1)

        def out_rows(b, c):
            return b * n_x_chunks + jnp.maximum(c - 1, 0)

    def seg_spec(seg):
        return pl.BlockSpec((CHUNK, width), lambda b, h, c, seg=seg: (row_block(b, c), seg * seg_blocks + h))

    in_specs = [pl.BlockSpec((1, hps, 8, HEAD_W), lambda b, h, c: (1 if backward else 0, h, 0, 0)),
                seg_spec(0), seg_spec(1), seg_spec(2)]
    operands = [decay_tab, proj, proj, proj]
    if backward:
        in_specs += [seg_spec(3), pl.BlockSpec((CHUNK, width), lambda b, h, c: (out_rows(b, c), h))]
        operands += [proj, fwd]
    return pl.pallas_call(
        functools.partial(_ret_kernel, heads_per_step=hps, backward=backward),
        grid=(batch, H_RET // hps, n_steps),
        in_specs=in_specs,
        out_specs=pl.BlockSpec((CHUNK, width), lambda b, h, c: (out_rows(b, c), h)),
        out_shape=jax.ShapeDtypeStruct((batch * seq, H_RET * HEAD_W),
                                       jnp.bfloat16 if backward else jnp.float32),
        scratch_shapes=[pltpu.VMEM((hps, HEAD_W, HEAD_W), jnp.float32)],
        compiler_params=_params("parallel", "parallel", "arbitrary"),
        name="retention_bwd" if backward else "retention_fwd",
    )(*operands)


def _diff_attn_kernel(q_ref, kx_ref, vx_ref, km_ref, vm_ref, lam_ref, subln_ref, slope_ref, o_ref,
                      *, tq, kb, n_kb, lam_init):
    qi = pl.program_id(2)
    slope = slope_ref[0][0:1, 0:1]
    q_scale = DIFF_HD ** -0.5
    q = (q_ref[...].astype(jnp.float32) * q_scale).astype(jnp.bfloat16)
    q1 = q[:, :DIFF_HD]
    q2 = q[:, DIFF_HD:]
    q_base = (qi * tq).astype(jnp.float32)

    def scores(qm, km):
        return lax.dot_general(qm, km, (((1,), (1,)), ((), ())), preferred_element_type=jnp.float32)

    def update(carry, s, v):
        m, l, acc = carry
        m_new = jnp.maximum(m, jnp.max(s, axis=-1, keepdims=True))
        alpha = jnp.exp(m - m_new)
        p = jnp.exp(s - m_new)
        l = alpha * l + jnp.sum(p, axis=-1, keepdims=True)
        acc = alpha * acc + jnp.dot(p.astype(jnp.bfloat16), v, preferred_element_type=jnp.float32)
        return m_new, l, acc

    def init():
        return (jnp.full((tq, 1), NEG_BIG, jnp.float32), jnp.zeros((tq, 1), jnp.float32),
                jnp.zeros((tq, HEAD_W), jnp.float32))

    rel = (lax.broadcasted_iota(jnp.int32, (tq, kb), 0)
           - lax.broadcasted_iota(jnp.int32, (tq, kb), 1)).astype(jnp.float32)

    def body(j, carry):
        c1, c2 = carry
        start = pl.multiple_of(j * kb, kb)
        k = kx_ref[pl.ds(start, kb), :]
        v = vx_ref[pl.ds(start, kb), :]
        bias = -slope * jnp.abs(rel + (q_base - (j * kb).astype(jnp.float32)))
        c1 = update(c1, scores(q1, k[:, :DIFF_HD]) + bias, v)
        c2 = update(c2, scores(q2, k[:, DIFF_HD:]) + bias, v)
        return c1, c2

    c1, c2 = lax.fori_loop(0, n_kb, body, (init(), init()))

    rel_m = (lax.broadcasted_iota(jnp.int32, (tq, CHUNK), 0)
             - lax.broadcasted_iota(jnp.int32, (tq, CHUNK), 1)).astype(jnp.float32)
    is_token = lax.broadcasted_iota(jnp.int32, (tq, CHUNK), 1) >= (CHUNK - N_META)
    bias_m = jnp.where(is_token, -slope * (rel_m + (q_base + float(CHUNK))), NEG_BIG)
    km = km_ref[...]
    vm = vm_ref[...]
    _, l1, acc1 = update(c1, jnp.where(is_token, scores(q1, km[:, :DIFF_HD]), 0.0) + bias_m, vm)
    _, l2, acc2 = update(c2, jnp.where(is_token, scores(q2, km[:, DIFF_HD:]), 0.0) + bias_m, vm)

    lp = lam_ref[...]
    lam = (jnp.exp(jnp.sum(lp[0:1] * lp[1:2], axis=-1, keepdims=True))
           - jnp.exp(jnp.sum(lp[2:3] * lp[3:4], axis=-1, keepdims=True)) + lam_init)
    o = acc1 / l1 - lam * (acc2 / l2)
    o = o * lax.rsqrt(jnp.mean(o * o, axis=-1, keepdims=True) + EPS)
    o_ref[...] = (o * subln_ref[...] * (1.0 - lam_init)).astype(o_ref.dtype)


def _diff_attention(proj, lam_params, subln_g, *, batch, seq, lam_init):
    tq = _largest_divisor(seq, 256, CHUNK)
    kb = _largest_divisor(seq, 512, CHUNK)
    n_q = seq // tq
    heads_w = H_DIFF * HEAD_W
    q_off = 4 * heads_w // HEAD_W
    k_off = 5 * heads_w // HEAD_W
    v_off = 6 * heads_w // HEAD_W
    meta_block0 = batch * seq // CHUNK
    slopes = 2.0 ** (-8.0 * np.arange(1, H_DIFF + 1, dtype=np.float32) / H_DIFF)
    slope_tab = jnp.asarray(np.broadcast_to(slopes[:, None, None], (H_DIFF, 8, CHUNK)), jnp.float32)
    return pl.pallas_call(
        functools.partial(_diff_attn_kernel, tq=tq, kb=kb, n_kb=seq // kb, lam_init=lam_init),
        grid=(batch, H_DIFF, n_q),
        in_specs=[
            pl.BlockSpec((tq, HEAD_W), lambda b, h, i: (b * n_q + i, q_off + h)),
            pl.BlockSpec((seq, HEAD_W), lambda b, h, i: (b, k_off + h)),
            pl.BlockSpec((seq, HEAD_W), lambda b, h, i: (b, v_off + h)),
            pl.BlockSpec((CHUNK, HEAD_W), lambda b, h, i: (meta_block0 + b, k_off + h)),
            pl.BlockSpec((CHUNK, HEAD_W), lambda b, h, i: (meta_block0 + b, v_off + h)),
            pl.BlockSpec((4, DIFF_HD), lambda b, h, i: (0, 0)),
            pl.BlockSpec((1, HEAD_W), lambda b, h, i: (0, 0)),
            pl.BlockSpec((1, 8, CHUNK), lambda b, h, i: (h, 0, 0)),
        ],
        out_specs=pl.BlockSpec((tq, HEAD_W), lambda b, h, i: (b * n_q + i, h)),
        out_shape=jax.ShapeDtypeStruct((batch * seq, heads_w), jnp.bfloat16),
        compiler_params=_params("parallel", "parallel", "arbitrary"),
        name="diff_attention",
    )(proj, proj, proj, proj, proj, lam_params, subln_g.reshape(1, HEAD_W), slope_tab)


def kernel(x, meta_tokens, norm_mix_g, w_in, ret_log_decay, diff_lambda, diff_subln_g,
           w_branch_ret, w_branch_diff, w_out, norm_ffn_g, w_ffn_gate, w_ffn_up, w_ffn_down,
           norm_final_g):
    batch, seq, d = x.shape
    depth = w_in.shape[0]
    assert seq % CHUNK == 0 and d == H_RET * HEAD_W and depth == 1
    bf16 = jnp.bfloat16
    rows = batch * seq
    h = x.reshape(rows, d)
    meta_chunk = jnp.pad(meta_tokens.astype(x.dtype), ((CHUNK - N_META, 0), (0, 0)))
    d_ff = w_ffn_gate.shape[-1]
    d_ff_pad = -(-d_ff // 1024) * 1024
    bm = _largest_divisor(rows, 1024, CHUNK)

    layer = 0
    lam_init = 0.8 - 0.6 * math.exp(-0.3 * layer)
    u = _rmsnorm_mix(h, meta_chunk, norm_mix_g[layer], batch)
    all_rows = u.shape[0]
    proj = _matmul(u, (w_in[layer].astype(bf16),), bf16,
                   bm=_largest_divisor(all_rows, 1280, CHUNK), bn=1024, name="in_proj")

    decay_tab = jnp.broadcast_to(ret_log_decay[layer].astype(jnp.float32)[:, :, None, None],
                                 (2, H_RET, 8, HEAD_W))
    ret_fwd = _retention(proj, decay_tab, batch=batch, seq=seq)
    y_ret = _retention(proj, decay_tab, batch=batch, seq=seq, fwd=ret_fwd)
    y_diff = _diff_attention(proj, diff_lambda[layer].astype(jnp.float32),
                             diff_subln_g[layer].astype(jnp.float32),
                             batch=batch, seq=seq, lam_init=lam_init)

    bn = 512
    gate_a_off = 7 * d // bn
    gate_b_off = 8 * d // bn
    part = _matmul(y_ret, (w_branch_ret[layer].astype(bf16),), jnp.float32, mode="gate",
                   extras=((proj, gate_a_off),), bm=bm, bn=bn, name="branch_ret")
    mixed = _matmul(y_diff, (w_branch_diff[layer].astype(bf16),), bf16, mode="gate_add",
                    extras=((proj, gate_b_off), (part, 0)), bm=bm, bn=bn, name="branch_diff")
    h = _matmul(mixed, (w_out[layer].astype(bf16),), jnp.float32, mode="add",
                extras=((h, 0),), bm=bm, bn=bn, name="mix_out")

    u = _rmsnorm_rows(h, norm_ffn_g[layer], bf16)
    pad_cols = ((0, 0), (0, d_ff_pad - d_ff))
    act = _matmul(u, (jnp.pad(w_ffn_gate[layer], pad_cols).astype(bf16),
                      jnp.pad(w_ffn_up[layer], pad_cols).astype(bf16)), bf16,
                  mode="swiglu", bm=bm, bn=bn, name="ffn_gate_up")
    w_down = jnp.pad(w_ffn_down[layer], ((0, d_ff_pad - d_ff), (0, 0))).astype(bf16)
    h = _matmul(act, (w_down,), jnp.float32, mode="add", extras=((h, 0),),
                bm=bm, bn=bn, bk=_largest_divisor(d_ff_pad, 2816, CHUNK), name="ffn_down")

    out = _rmsnorm_rows(h, norm_final_g, x.dtype)
    return out.reshape(batch, seq, d)
```

```python
import functools
import math

import jax
import jax.numpy as jnp
import numpy as np
from jax import lax
from jax.experimental import pallas as pl
from jax.experimental.pallas import tpu as pltpu

N_META = 16
CHUNK = 128
H_RET = 16
H_DIFF = 16
HEAD_W = 256
DIFF_HD = 128
EPS = 1e-6
NEG_BIG = -1e30
LOG2E = math.log2(math.e)

V7X_VMEM_BYTES = 64 * 1024 * 1024
VMEM_LIMIT = V7X_VMEM_BYTES - 8 * 1024 * 1024


def _largest_divisor(n, target, multiple):
    d = (min(n, target) // multiple) * multiple
    while d > multiple and n % d:
        d -= multiple
    assert d >= multiple and n % d == 0, (n, target, multiple)
    return d


def _params(*semantics):
    return pltpu.CompilerParams(dimension_semantics=semantics, vmem_limit_bytes=VMEM_LIMIT)


def _rmsnorm_rows_kernel(x_ref, g_ref, o_ref):
    x = x_ref[...].astype(jnp.float32)
    y = x * lax.rsqrt(jnp.mean(x * x, axis=-1, keepdims=True) + EPS)
    o_ref[...] = (y * g_ref[...]).astype(o_ref.dtype)


def _rmsnorm_mix_kernel(x_ref, meta_ref, g_ref, o_ref, *, n_x_blocks):
    i = pl.program_id(0)

    def norm(v):
        y = v * lax.rsqrt(jnp.mean(v * v, axis=-1, keepdims=True) + EPS)
        return (y * g_ref[...]).astype(o_ref.dtype)

    @pl.when(i < n_x_blocks)
    def _():
        o_ref[...] = norm(x_ref[...])

    @pl.when(i >= n_x_blocks)
    def _():
        o_ref[...] = norm(meta_ref[...])


def _rmsnorm_mix(x2d, meta_chunk, g, batch):
    rows, d = x2d.shape
    n_x_blocks = rows // CHUNK
    return pl.pallas_call(
        functools.partial(_rmsnorm_mix_kernel, n_x_blocks=n_x_blocks),
        grid=(n_x_blocks + batch,),
        in_specs=[
            pl.BlockSpec((CHUNK, d), lambda i: (jnp.minimum(i, n_x_blocks - 1), 0)),
            pl.BlockSpec((CHUNK, d), lambda i: (0, 0)),
            pl.BlockSpec((1, d), lambda i: (0, 0)),
        ],
        out_specs=pl.BlockSpec((CHUNK, d), lambda i: (i, 0)),
        out_shape=jax.ShapeDtypeStruct((rows + batch * CHUNK, d), jnp.bfloat16),
        compiler_params=_params("arbitrary"),
        name="rmsnorm_mix",
    )(x2d, meta_chunk, g.reshape(1, d))


def _rmsnorm_rows(x2d, g, out_dtype):
    rows, d = x2d.shape
    br = _largest_divisor(rows, 512, 8)
    return pl.pallas_call(
        _rmsnorm_rows_kernel,
        grid=(rows // br,),
        in_specs=[pl.BlockSpec((br, d), lambda i: (i, 0)),
                  pl.BlockSpec((1, d), lambda i: (0, 0))],
        out_specs=pl.BlockSpec((br, d), lambda i: (i, 0)),
        out_shape=jax.ShapeDtypeStruct((rows, d), out_dtype),
        compiler_params=_params("arbitrary"),
        name="rmsnorm_rows",
    )(x2d, g.reshape(1, d))


def _sigmoid(x):
    return 1.0 / (1.0 + jnp.exp(-x))


def _mm_kernel(*refs, mode, nk):
    if mode == "swiglu":
        a_ref, b0_ref, b1_ref, o_ref = refs
        a = a_ref[...]
        gate = jnp.dot(a, b0_ref[...], preferred_element_type=jnp.float32)
        up = jnp.dot(a, b1_ref[...], preferred_element_type=jnp.float32)
        o_ref[...] = (gate * _sigmoid(gate) * up).astype(o_ref.dtype)
        return

    n_extra = {"plain": 0, "gate": 1, "add": 1, "gate_add": 2}[mode]
    a_ref, b_ref = refs[:2]
    extra = refs[2:2 + n_extra]
    o_ref = refs[2 + n_extra]

    def epilogue(acc):
        if mode == "plain":
            r = acc
        elif mode == "gate":
            r = _sigmoid(extra[0][...].astype(jnp.float32)) * acc
        elif mode == "add":
            r = extra[0][...].astype(jnp.float32) + acc
        else:
            r = extra[1][...].astype(jnp.float32) + _sigmoid(extra[0][...].astype(jnp.float32)) * acc
        o_ref[...] = r.astype(o_ref.dtype)

    part = jnp.dot(a_ref[...], b_ref[...], preferred_element_type=jnp.float32)
    if nk == 1:
        epilogue(part)
        return

    acc_ref = refs[3 + n_extra]
    k = pl.program_id(2)

    @pl.when(k == 0)
    def _():
        acc_ref[...] = part

    @pl.when(jnp.logical_and(k > 0, k < nk - 1))
    def _():
        acc_ref[...] += part

    @pl.when(k == nk - 1)
    def _():
        epilogue(acc_ref[...] + part)


def _matmul(a, bs, out_dtype, *, mode="plain", extras=(), bm, bn, bk=None, name):
    m, kdim = a.shape
    n = bs[0].shape[1]
    bk = kdim if bk is None else bk
    nk = kdim // bk
    assert m % bm == 0 and n % bn == 0 and kdim % bk == 0
    in_specs = [pl.BlockSpec((bm, bk), lambda i, j, k: (i, k))]
    in_specs += [pl.BlockSpec((bk, bn), lambda i, j, k: (k, j)) for _ in bs]
    for _, off in extras:
        in_specs.append(pl.BlockSpec((bm, bn), lambda i, j, k, off=off: (i, off + j)))
    scratch = [pltpu.VMEM((bm, bn), jnp.float32)] if nk > 1 else []
    return pl.pallas_call(
        functools.partial(_mm_kernel, mode=mode, nk=nk),
        grid=(m // bm, n // bn, nk),
        in_specs=in_specs,
        out_specs=pl.BlockSpec((bm, bn), lambda i, j, k: (i, j)),
        out_shape=jax.ShapeDtypeStruct((m, n), out_dtype),
        scratch_shapes=scratch,
        compiler_params=_params("parallel", "parallel", "arbitrary"),
        name=name,
    )(a, *bs, *[e for e, _ in extras])


def _ret_kernel(*refs, heads_per_step, backward):
    if backward:
        decay_ref, q_ref, k_ref, v_ref, gate_ref, fwd_ref, o_ref, s_ref = refs
    else:
        decay_ref, q_ref, k_ref, v_ref, o_ref, s_ref = refs
    step = pl.program_id(2)

    @pl.when(step == 0)
    def _():
        s_ref[...] = jnp.zeros_like(s_ref)

    row = lax.broadcasted_iota(jnp.int32, (CHUNK, CHUNK), 0).astype(jnp.float32)
    col = lax.broadcasted_iota(jnp.int32, (CHUNK, CHUNK), 1).astype(jnp.float32)
    dist = (col - row) if backward else (row - col)
    keep = (dist > 0) if backward else (dist >= 0)
    pos = lax.broadcasted_iota(jnp.int32, (CHUNK, HEAD_W), 0).astype(jnp.float32)
    k_scale = HEAD_W ** -0.5

    for g in range(heads_per_step):
        cols = slice(g * HEAD_W, (g + 1) * HEAD_W)
        log_g = -jnp.exp(decay_ref[0, g])
        log_g_row = log_g[0:1, :]
        intra = jnp.where(keep, jnp.exp(log_g_row[:, :CHUNK] * jnp.maximum(dist, 0.0)), 0.0) * k_scale
        if backward:
            q_dec = jnp.exp(log_g_row * (CHUNK - pos))
            k_dec = jnp.exp(log_g_row * pos) * k_scale
        else:
            q_dec = jnp.exp(log_g_row * (pos + 1.0))
            k_dec = jnp.exp(log_g_row * (CHUNK - 1.0 - pos)) * k_scale
        c_dec = jnp.exp(log_g_row * float(CHUNK))

        q = q_ref[:, cols]
        k = k_ref[:, cols]
        v = v_ref[:, cols]
        state = s_ref[g]
        scores = lax.dot_general(q, k, (((1,), (1,)), ((), ())),
                                 preferred_element_type=jnp.float32) * intra
        out = jnp.dot(scores.astype(jnp.bfloat16), v, preferred_element_type=jnp.float32)
        out += jnp.dot((q.astype(jnp.float32) * q_dec).astype(jnp.bfloat16),
                       state.astype(jnp.bfloat16), preferred_element_type=jnp.float32)
        k_w = (k.astype(jnp.float32) * k_dec).astype(jnp.bfloat16)
        s_ref[g] = state * c_dec + lax.dot_general(k_w, v, (((0,), (0,)), ((), ())),
                                                   preferred_element_type=jnp.float32)
        if backward:
            tot = out + fwd_ref[:, cols]
            y = tot * lax.rsqrt(jnp.mean(tot * tot, axis=-1, keepdims=True) + EPS)
            gate = gate_ref[:, cols].astype(jnp.float32)
            o_ref[:, cols] = (gate * _sigmoid(gate) * y).astype(o_ref.dtype)
        else:
            @pl.when(step > 0)
            def _():
                o_ref[:, cols] = out


def _retention(proj, decay_tab, *, batch, seq, fwd=None):
    backward = fwd is not None
    n_x_chunks = seq // CHUNK
    hps = 4
    width = hps * HEAD_W
    seg_blocks = (H_RET * HEAD_W) // width
    meta_block0 = batch * n_x_chunks

    if backward:
        n_steps = n_x_chunks

        def row_block(b, c):
            return b * n_x_chunks + (n_x_chunks - 1 - c)

        out_rows = row_block
    else:
        n_steps = n_x_chunks + 1

        def row_block(b, c):
            return jnp.where(c == 0, meta_block0 + b, b * n_x_chunks + c - 1)

        def out_rows(b, c):
            return b * n_x_chunks + jnp.maximum(c - 1, 0)

    def seg_spec(seg):
        return pl.BlockSpec((CHUNK, width), lambda b, h, c, seg=seg: (row_block(b, c), seg * seg_blocks + h))

    in_specs = [pl.BlockSpec((1, hps, 8, HEAD_W), lambda b, h, c: (1 if backward else 0, h, 0, 0)),
                seg_spec(0), seg_spec(1), seg_spec(2)]
    operands = [decay_tab, proj, proj, proj]
    if backward:
        in_specs += [seg_spec(3), pl.BlockSpec((CHUNK, width), lambda b, h, c: (out_rows(b, c), h))]
        operands += [proj, fwd]
    return pl.pallas_call(
        functools.partial(_ret_kernel, heads_per_step=hps, backward=backward),
        grid=(batch, H_RET // hps, n_steps),
        in_specs=in_specs,
        out_specs=pl.BlockSpec((CHUNK, width), lambda b, h, c: (out_rows(b, c), h)),
        out_shape=jax.ShapeDtypeStruct((batch * seq, H_RET * HEAD_W),
                                       jnp.bfloat16 if backward else jnp.float32),
        scratch_shapes=[pltpu.VMEM((hps, HEAD_W, HEAD_W), jnp.float32)],
        compiler_params=_params("parallel", "parallel", "arbitrary"),
        name="retention_bwd" if backward else "retention_fwd",
    )(*operands)


def _diff_attn_kernel(q_ref, kx_ref, vx_ref, km_ref, vm_ref, lam_ref, subln_ref, slope_ref, o_ref,
                      s_ref, p_ref, bias_ref, m_ref, l_ref, acc_ref,
                      *, tq, kb, n_kb, rc, lam_init):
    f32 = jnp.float32
    qi = pl.program_id(2)
    slope = slope_ref[0][0:1, 0:1] * LOG2E
    q = (q_ref[...].astype(f32) * (DIFF_HD ** -0.5 * LOG2E)).astype(jnp.bfloat16)
    q_maps = (q[:, :DIFF_HD], q[:, DIFF_HD:])
    j_diag = (qi * tq) // kb

    rel = (lax.broadcasted_iota(jnp.int32, (tq, kb), 0)
           - lax.broadcasted_iota(jnp.int32, (tq, kb), 1)).astype(f32)

    @pl.when(qi == 0)
    def _():
        bias_ref[0] = -slope * rel
        bias_ref[1] = slope * rel

    bias_ref[2] = -slope * jnp.abs(rel + (qi * tq - j_diag * kb).astype(f32))
    m_ref[...] = jnp.full_like(m_ref, NEG_BIG)
    l_ref[...] = jnp.zeros_like(l_ref)
    acc_ref[...] = jnp.zeros_like(acc_ref)

    def scores(qm, km):
        return lax.dot_general(qm, km, (((1,), (1,)), ((), ())), preferred_element_type=f32)

    def softmax_rows(row0, t, const):
        rows = pl.ds(row0, rc)
        width = t.shape[1] // CHUNK
        m_old = m_ref[rows, :]
        m_new = jnp.maximum(m_old, jnp.max(t, axis=-1, keepdims=True) + const)
        alpha = jnp.exp2(m_old - m_new)
        p = jnp.exp2(t - jnp.tile(m_new - const, (1, width)))
        psum = p[:, :CHUNK]
        for w in range(1, width):
            psum = psum + p[:, w * CHUNK:(w + 1) * CHUNK]
        l_ref[rows, :] = alpha * l_ref[rows, :] + psum
        m_ref[rows, :] = m_new
        acc_ref[rows, :] = acc_ref[rows, :] * jnp.tile(alpha, (1, HEAD_W // CHUNK))
        return p.astype(jnp.bfloat16)

    def body(j, carry):
        start = pl.multiple_of(j * kb, kb)
        tile = jnp.where(j < j_diag, 0, jnp.where(j > j_diag, 1, 2))
        dist = jnp.where(j == j_diag, 0, jnp.abs(qi * tq - j * kb)).astype(f32)
        const = -slope * dist
        for mp in range(2):
            s_ref[mp] = scores(q_maps[mp], kx_ref[pl.ds(start, kb), mp * DIFF_HD:(mp + 1) * DIFF_HD])
        for mp in range(2):
            for r0 in range(0, tq, rc):
                t = s_ref[mp, r0:r0 + rc, :] + bias_ref[tile, pl.ds(r0, rc), :]
                p_ref[mp * tq + r0:mp * tq + r0 + rc, :] = softmax_rows(mp * tq + r0, t, const)
        acc_ref[...] += jnp.dot(p_ref[...], vx_ref[pl.ds(start, kb), :], preferred_element_type=f32)
        return carry

    lax.fori_loop(0, n_kb, body, 0)

    rel_m = (lax.broadcasted_iota(jnp.int32, (tq, CHUNK), 0)
             - lax.broadcasted_iota(jnp.int32, (tq, CHUNK), 1)).astype(f32)
    is_token = lax.broadcasted_iota(jnp.int32, (tq, CHUNK), 1) >= (CHUNK - N_META)
    bias_m = -slope * (rel_m + (qi * tq + CHUNK).astype(f32))
    zero = jnp.zeros((1, 1), f32)
    for mp in range(2):
        t_m = jnp.where(is_token, scores(q_maps[mp], km_ref[:, mp * DIFF_HD:(mp + 1) * DIFF_HD]) + bias_m,
                        NEG_BIG)
        for r0 in range(0, tq, rc):
            p_ref[mp * tq + r0:mp * tq + r0 + rc, :CHUNK] = softmax_rows(mp * tq + r0, t_m[r0:r0 + rc], zero)
    acc = acc_ref[...] + jnp.dot(p_ref[:, :CHUNK], vm_ref[...], preferred_element_type=f32)

    lp = lam_ref[...]
    lam = (jnp.exp(jnp.sum(lp[0:1] * lp[1:2], axis=-1, keepdims=True))
           - jnp.exp(jnp.sum(lp[2:3] * lp[3:4], axis=-1, keepdims=True)) + lam_init)
    l = jnp.sum(l_ref[...], axis=-1, keepdims=True)
    o = acc[:tq] / l[:tq] - lam * (acc[tq:] / l[tq:])
    o = o * lax.rsqrt(jnp.mean(o * o, axis=-1, keepdims=True) + EPS)
    o_ref[...] = (o * subln_ref[...] * (1.0 - lam_init)).astype(o_ref.dtype)


def _diff_attention(proj, lam_params, subln_g, *, batch, seq, lam_init):
    tq = _largest_divisor(seq, 256, CHUNK)
    kb = _largest_divisor(seq, 1024, tq)
    rc = 16
    n_q = seq // tq
    heads_w = H_DIFF * HEAD_W
    q_off = 4 * heads_w // HEAD_W
    k_off = 5 * heads_w // HEAD_W
    v_off = 6 * heads_w // HEAD_W
    meta_block0 = batch * seq // CHUNK
    slopes = 2.0 ** (-8.0 * np.arange(1, H_DIFF + 1, dtype=np.float32) / H_DIFF)
    slope_tab = jnp.asarray(np.broadcast_to(slopes[:, None, None], (H_DIFF, 8, CHUNK)), jnp.float32)
    return pl.pallas_call(
        functools.partial(_diff_attn_kernel, tq=tq, kb=kb, n_kb=seq // kb, rc=rc, lam_init=lam_init),
        grid=(batch, H_DIFF, n_q),
        in_specs=[
            pl.BlockSpec((tq, HEAD_W), lambda b, h, i: (b * n_q + i, q_off + h)),
            pl.BlockSpec((seq, HEAD_W), lambda b, h, i: (b, k_off + h)),
            pl.BlockSpec((seq, HEAD_W), lambda b, h, i: (b, v_off + h)),
            pl.BlockSpec((CHUNK, HEAD_W), lambda b, h, i: (meta_block0 + b, k_off + h)),
            pl.BlockSpec((CHUNK, HEAD_W), lambda b, h, i: (meta_block0 + b, v_off + h)),
            pl.BlockSpec((4, DIFF_HD), lambda b, h, i: (0, 0)),
            pl.BlockSpec((1, HEAD_W), lambda b, h, i: (0, 0)),
            pl.BlockSpec((1, 8, CHUNK), lambda b, h, i: (h, 0, 0)),
        ],
        out_specs=pl.BlockSpec((tq, HEAD_W), lambda b, h, i: (b * n_q + i, h)),
        out_shape=jax.ShapeDtypeStruct((batch * seq, heads_w), jnp.bfloat16),
        scratch_shapes=[
            pltpu.VMEM((2, tq, kb), jnp.float32),
            pltpu.VMEM((2 * tq, kb), jnp.bfloat16),
            pltpu.VMEM((3, tq, kb), jnp.float32),
            pltpu.VMEM((2 * tq, CHUNK), jnp.float32),
            pltpu.VMEM((2 * tq, CHUNK), jnp.float32),
            pltpu.VMEM((2 * tq, HEAD_W), jnp.float32),
        ],
        compiler_params=_params("parallel", "parallel", "arbitrary"),
        name="diff_attention",
    )(proj, proj, proj, proj, proj, lam_params, subln_g.reshape(1, HEAD_W), slope_tab)


def kernel(x, meta_tokens, norm_mix_g, w_in, ret_log_decay, diff_lambda, diff_subln_g,
           w_branch_ret, w_branch_diff, w_out, norm_ffn_g, w_ffn_gate, w_ffn_up, w_ffn_down,
           norm_final_g):
    batch, seq, d = x.shape
    depth = w_in.shape[0]
    assert seq % CHUNK == 0 and d == H_RET * HEAD_W and depth == 1
    bf16 = jnp.bfloat16
    rows = batch * seq
    h = x.reshape(rows, d)
    meta_chunk = jnp.pad(meta_tokens.astype(x.dtype), ((CHUNK - N_META, 0), (0, 0)))
    d_ff = w_ffn_gate.shape[-1]
    d_ff_pad = -(-d_ff // 1024) * 1024
    bm = _largest_divisor(rows, 1024, CHUNK)

    layer = 0
    lam_init = 0.8 - 0.6 * math.exp(-0.3 * layer)
    u = _rmsnorm_mix(h, meta_chunk, norm_mix_g[layer], batch)
    all_rows = u.shape[0]
    proj = _matmul(u, (w_in[layer].astype(bf16),), bf16,
                   bm=_largest_divisor(all_rows, 1280, CHUNK), bn=1024, name="in_proj")

    decay_tab = jnp.broadcast_to(ret_log_decay[layer].astype(jnp.float32)[:, :, None, None],
                                 (2, H_RET, 8, HEAD_W))
    ret_fwd = _retention(proj, decay_tab, batch=batch, seq=seq)
    y_ret = _retention(proj, decay_tab, batch=batch, seq=seq, fwd=ret_fwd)
    y_diff = _diff_attention(proj, diff_lambda[layer].astype(jnp.float32),
                             diff_subln_g[layer].astype(jnp.float32),
                             batch=batch, seq=seq, lam_init=lam_init)

    bn = 512
    gate_a_off = 7 * d // bn
    gate_b_off = 8 * d // bn
    part = _matmul(y_ret, (w_branch_ret[layer].astype(bf16),), jnp.float32, mode="gate",
                   extras=((proj, gate_a_off),), bm=bm, bn=bn, name="branch_ret")
    mixed = _matmul(y_diff, (w_branch_diff[layer].astype(bf16),), bf16, mode="gate_add",
                    extras=((proj, gate_b_off), (part, 0)), bm=bm, bn=bn, name="branch_diff")
    h = _matmul(mixed, (w_out[layer].astype(bf16),), jnp.float32, mode="add",
                extras=((h, 0),), bm=bm, bn=bn, name="mix_out")

    u = _rmsnorm_rows(h, norm_ffn_g[layer], bf16)
    pad_cols = ((0, 0), (0, d_ff_pad - d_ff))
    act = _matmul(u, (jnp.pad(w_ffn_gate[layer], pad_cols).astype(bf16),
                      jnp.pad(w_ffn_up[layer], pad_cols).astype(bf16)), bf16,
                  mode="swiglu", bm=bm, bn=bn, name="ffn_gate_up")
    w_down = jnp.pad(w_ffn_down[layer], ((0, d_ff_pad - d_ff), (0, 0))).astype(bf16)
    h = _matmul(act, (w_down,), jnp.float32, mode="add", extras=((h, 0),),
                bm=bm, bn=bn, bk=_largest_divisor(d_ff_pad, 2816, CHUNK), name="ffn_down")

    out = _rmsnorm_rows(h, norm_final_g, x.dtype)
    return out.reshape(batch, seq, d)
```
